```python
import math
import jax, jax.numpy as jnp
from jax import lax
import numpy as np

D_MODEL = 1024
BATCH = 8
SEQ = 4096
DEPTH = 2

M_HEADS = 4
M_V_WIDTH = 2 * D_MODEL
M_QK_WIDTH = D_MODEL
M_DV = M_V_WIDTH // M_HEADS
M_DK = M_QK_WIDTH // M_HEADS
M_CHUNK = 128
CONV_K = 4
A_HEADS = D_MODEL // 128
A_DH = 64
A_QK_WIDTH = A_HEADS * 2 * A_DH
A_V_WIDTH = A_HEADS * 2 * A_DH
Q_BLOCK = 128
REL_BUCKETS = 32
REL_MAX_DIST = 128
EPS = 1e-6

SPLITS = (M_QK_WIDTH, M_QK_WIDTH, M_V_WIDTH, M_V_WIDTH, M_V_WIDTH, M_HEADS, M_HEADS,
          A_QK_WIDTH, A_QK_WIDTH, A_V_WIDTH, A_V_WIDTH, D_MODEL, D_MODEL)
N_IN = 3 * M_QK_WIDTH - M_QK_WIDTH + 3 * M_V_WIDTH + 2 * M_HEADS + 2 * A_QK_WIDTH + 2 * A_V_WIDTH + 2 * D_MODEL

kernel_name = "hybrid_mlstm_diffattn_gated_block"


def rmsnorm(x, g):
    xf = x.astype(jnp.float32)
    y = xf * lax.rsqrt(jnp.mean(xf * xf, axis=-1, keepdims=True) + EPS)
    return (y * g.astype(jnp.float32)).astype(x.dtype)


def head_rmsnorm(x, n_heads, g):
    B, S, W = x.shape
    xf = x.astype(jnp.float32).reshape(B, S, n_heads, W // n_heads)
    xf = xf * lax.rsqrt(jnp.mean(xf * xf, axis=-1, keepdims=True) + EPS)
    return (xf.reshape(B, S, W) * g.astype(jnp.float32)).astype(x.dtype)


def causal_dwconv(x, w, b):
    S = x.shape[1]
    xp = jnp.pad(x, ((0, 0), (CONV_K - 1, 0), (0, 0)))
    y = xp[:, CONV_K - 1:CONV_K - 1 + S] * w[CONV_K - 1] + b
    for j in range(CONV_K - 1):
        y = y + xp[:, j:j + S] * w[j]
    return y


def t5_bucket(n):
    max_exact = REL_BUCKETS // 2
    nf = jnp.maximum(n, 1).astype(jnp.float32)
    large = max_exact + (jnp.log(nf / max_exact) / math.log(REL_MAX_DIST / max_exact)
                         * (REL_BUCKETS - max_exact)).astype(jnp.int32)
    large = jnp.minimum(large, REL_BUCKETS - 1)
    return jnp.where(n < max_exact, n, large)


def mlstm_chunkwise(q, k, v, i_pre, f_pre):
    in_dtype = v.dtype
    B, S, H, dk = q.shape
    dv = v.shape[-1]
    nc = S // M_CHUNK
    L = M_CHUNK
    f32 = jnp.float32

    def chunks4(t):
        return t.astype(f32).reshape(B, nc, L, H, t.shape[-1]).transpose(1, 0, 3, 2, 4)

    def chunks3(t):
        return t.astype(f32).reshape(B, nc, L, H).transpose(1, 0, 3, 2)

    qc = chunks4(q)
    kc = chunks4(k) * (dk ** -0.5)
    vc = chunks4(v)
    ic = chunks3(i_pre)
    lfc = jax.nn.log_sigmoid(chunks3(f_pre))
    causal = jnp.tril(jnp.ones((L, L), dtype=bool))

    def step(carry, xs):
        C, n, m = carry
        qx, kx, vx, ix, lfx = xs
        b = jnp.cumsum(lfx, axis=-1)
        D = b[..., :, None] - b[..., None, :] + ix[..., None, :]
        D = jnp.where(causal, D, -jnp.inf)
        inter = b + m[..., None]
        m_t = jnp.maximum(inter, jnp.max(D, axis=-1))
        w_inter = jnp.exp(inter - m_t)
        s = jnp.einsum('bhtd,bhsd->bhts', qx, kx) * jnp.exp(D - m_t[..., None])
        num = (w_inter[..., None] * jnp.einsum('bhvd,bhtd->bhtv', C, qx)
               + jnp.einsum('bhts,bhsv->bhtv', s, vx))
        den = w_inter * jnp.einsum('bhd,bhtd->bht', n, qx) + jnp.sum(s, axis=-1)
        h = num / jnp.maximum(jnp.abs(den), jnp.exp(-m_t))[..., None]
        bL = b[..., -1]
        g = bL[..., None] - b + ix
        m_new = jnp.maximum(bL + m, jnp.max(g, axis=-1))
        decay = jnp.exp(bL + m - m_new)
        wk = jnp.exp(g - m_new[..., None])
        C_new = decay[..., None, None] * C + jnp.einsum('bhsv,bhsd->bhvd', vx * wk[..., None], kx)
        n_new = decay[..., None] * n + jnp.einsum('bhs,bhsd->bhd', wk, kx)
        return (C_new, n_new, m_new), h

    init = (jnp.zeros((B, H, dv, dk), f32), jnp.zeros((B, H, dk), f32), jnp.zeros((B, H), f32))
    _, hs = lax.scan(step, init, (qc, kc, vc, ic, lfc))
    return hs.transpose(1, 0, 3, 2, 4).reshape(B, S, H, dv).astype(in_dtype)


def diff_attention(q, k, v, lam, rel_bias):
    in_dtype = v.dtype
    f32 = jnp.float32
    S = q.shape[1]
    qh = q.astype(f32).transpose(0, 2, 3, 1, 4) * (A_DH ** -0.5)
    kh = k.astype(f32).transpose(0, 2, 3, 1, 4)
    vh = v.astype(f32).transpose(0, 2, 1, 3)
    table = rel_bias.astype(f32)
    outs = []
    for blk in range(S // Q_BLOCK):
        q0 = blk * Q_BLOCK
        end = q0 + Q_BLOCK
        rel = jnp.arange(q0, end)[:, None] - jnp.arange(end)[None, :]
        bias = table[t5_bucket(jnp.maximum(rel, 0))]
        bias = jnp.where((rel >= 0)[..., None], bias, -jnp.inf).transpose(2, 0, 1)
        logits = jnp.einsum('bhmqd,bhmkd->bhmqk', qh[:, :, :, q0:end], kh[:, :, :, :end])
        p = jax.nn.softmax(logits + bias[None, :, None], axis=-1)
        a = p[:, :, 0] - lam * p[:, :, 1]
        outs.append(jnp.einsum('bhqk,bhkd->bhqd', a, vh[:, :, :end]))
    o = jnp.concatenate(outs, axis=2)
    return o.transpose(0, 2, 1, 3).astype(in_dtype)


def setup_inputs(seed: int = 0) -> dict:
    key = jax.random.key(seed)
    ks = jax.random.split(key, 24)
    nrm = jax.random.normal
    f32 = jnp.float32
    x = nrm(ks[0], (BATCH, SEQ, D_MODEL), f32)
    c = nrm(ks[1], (BATCH, D_MODEL), f32)
    norm_g = 1.0 + 0.05 * nrm(ks[2], (DEPTH, D_MODEL), f32)
    w_ada = 0.5 * D_MODEL ** -0.5 * nrm(ks[3], (DEPTH, D_MODEL, 3 * D_MODEL), f32)
    b_ada = 0.02 * nrm(ks[4], (DEPTH, 3 * D_MODEL), f32)
    w_in = D_MODEL ** -0.5 * nrm(ks[5], (DEPTH, D_MODEL, N_IN), f32)
    b_i = 0.1 * nrm(ks[6], (DEPTH, M_HEADS), f32)
    b_f = jnp.linspace(3.0, 6.0, M_HEADS, dtype=f32)[None] + 0.1 * nrm(ks[7], (DEPTH, M_HEADS), f32)
    b_if = jnp.concatenate([b_i, b_f], axis=-1)
    conv_w = CONV_K ** -0.5 * nrm(ks[8], (DEPTH, CONV_K, 2 * M_QK_WIDTH), f32)
    conv_b = 0.02 * nrm(ks[9], (DEPTH, 2 * M_QK_WIDTH), f32)
    mnorm_g = 1.0 + 0.05 * nrm(ks[10], (DEPTH, M_V_WIDTH), f32)
    w_proj_m = M_V_WIDTH ** -0.5 * nrm(ks[11], (DEPTH, M_V_WIDTH, D_MODEL), f32)
    lam_q1 = 0.1 * nrm(ks[12], (DEPTH, A_DH), f32)
    lam_k1 = 0.1 * nrm(ks[13], (DEPTH, A_DH), f32)
    lam_q2 = 0.1 * nrm(ks[14], (DEPTH, A_DH), f32)
    lam_k2 = 0.1 * nrm(ks[15], (DEPTH, A_DH), f32)
    anorm_g = 1.0 + 0.05 * nrm(ks[16], (DEPTH, A_V_WIDTH), f32)
    w_proj_a = A_V_WIDTH ** -0.5 * nrm(ks[17], (DEPTH, A_V_WIDTH, D_MODEL), f32)
    w_out = D_MODEL ** -0.5 * nrm(ks[18], (DEPTH, D_MODEL, D_MODEL), f32)
    rel_bias = 0.5 * nrm(ks[19], (REL_BUCKETS, A_HEADS), f32)
    final_g = 1.0 + 0.05 * nrm(ks[20], (D_MODEL,), f32)
    return {"x": x, "c": c, "norm_g": norm_g, "w_ada": w_ada, "b_ada": b_ada, "w_in": w_in,
            "b_if": b_if, "conv_w": conv_w, "conv_b": conv_b, "mnorm_g": mnorm_g, "w_proj_m": w_proj_m,
            "lam_q1": lam_q1, "lam_k1": lam_k1, "lam_q2": lam_q2, "lam_k2": lam_k2, "anorm_g": anorm_g,
            "w_proj_a": w_proj_a, "w_out": w_out, "rel_bias": rel_bias, "final_g": final_g}


def reference(x, c, norm_g, w_ada, b_ada, w_in, b_if, conv_w, conv_b, mnorm_g, w_proj_m,
              lam_q1, lam_k1, lam_q2, lam_k2, anorm_g, w_proj_a, w_out, rel_bias, final_g):
    B, S, _ = x.shape
    split_idx = np.cumsum(np.array(SPLITS))[:-1].tolist()
    c_act = jax.nn.silu(c)
    for l in range(DEPTH):
        mod = c_act @ w_ada[l] + b_ada[l]
        shift, scale, gate = jnp.split(mod, 3, axis=-1)
        h = rmsnorm(x, norm_g[l]) * (1.0 + scale[:, None]) + shift[:, None]

        proj = h @ w_in[l]
        (qm, km, vm, om, zm, im, fm, qa, ka, va, za, gm, ga) = jnp.split(proj, split_idx, axis=-1)

        qk = jax.nn.silu(causal_dwconv(jnp.concatenate([qm, km], axis=-1), conv_w[l], conv_b[l]))
        qm, km = jnp.split(qk, 2, axis=-1)
        im = im + b_if[l, :M_HEADS]
        fm = fm + b_if[l, M_HEADS:]
        hm = mlstm_chunkwise(qm.reshape(B, S, M_HEADS, M_DK), km.reshape(B, S, M_HEADS, M_DK),
                             vm.reshape(B, S, M_HEADS, M_DV), im, fm)
        hm = jax.nn.sigmoid(om) * hm.reshape(B, S, M_V_WIDTH)
        hm = head_rmsnorm(hm, M_HEADS, mnorm_g[l]) * jax.nn.silu(zm)
        ym = hm @ w_proj_m[l]

        lam_init = 0.8 - 0.6 * math.exp(-0.3 * l)
        lam = (jnp.exp(jnp.sum(lam_q1[l].astype(jnp.float32) * lam_k1[l].astype(jnp.float32)))
               - jnp.exp(jnp.sum(lam_q2[l].astype(jnp.float32) * lam_k2[l].astype(jnp.float32)))
               + lam_init)
        ha = diff_attention(qa.reshape(B, S, A_HEADS, 2, A_DH), ka.reshape(B, S, A_HEADS, 2, A_DH),
                            va.reshape(B, S, A_HEADS, 2 * A_DH), lam, rel_bias)
        ha = head_rmsnorm(ha.reshape(B, S, A_V_WIDTH), A_HEADS, anorm_g[l]) * (1.0 - lam_init)
        ha = ha * jax.nn.silu(za)
        ya = ha @ w_proj_a[l]

        y = jax.nn.sigmoid(gm) * ym + jax.nn.sigmoid(ga) * ya
        x = x + gate[:, None] * (y @ w_out[l])
    return rmsnorm(x, final_g)
```

```python
import functools
import math

import numpy as np
import jax
import jax.numpy as jnp
from jax import lax
from jax.experimental import pallas as pl
from jax.experimental.pallas import tpu as pltpu

F32 = jnp.float32
BF16 = jnp.bfloat16

D_MODEL = 1024
M_HEADS = 4
M_DK = 256
M_DV = 512
M_QK_WIDTH = M_HEADS * M_DK
M_V_WIDTH = M_HEADS * M_DV
M_CHUNK = 128
CONV_K = 4
A_HEADS = 8
A_DH = 64
A_HEAD_WIDTH = 2 * A_DH
A_WIDTH = A_HEADS * A_HEAD_WIDTH
REL_BUCKETS = 32
REL_MAX_DIST = 128
EPS = 1e-6

COL_QM = 0
COL_KM = COL_QM + M_QK_WIDTH
COL_VM = COL_KM + M_QK_WIDTH
COL_OM = COL_VM + M_V_WIDTH
COL_ZM = COL_OM + M_V_WIDTH
COL_IF = COL_ZM + M_V_WIDTH
COL_QA = COL_ZM + M_V_WIDTH
COL_KA = COL_QA + A_WIDTH
COL_VA = COL_KA + A_WIDTH
COL_ZA = COL_VA + A_WIDTH
COL_GM = COL_ZA + A_WIDTH
COL_GA = COL_GM + D_MODEL
N_MAIN = COL_GA + D_MODEL
IF_PAD = 128

VMEM_LIMIT = 56 * 1024 * 1024

ADA_TN = 1024
PROJ_TM = 1024
PROJ_TN = 2048
MLSTM_T = 512
ATTN_T = 256
MERGE_TM = 512


def _silu(x):
    return x * jax.nn.sigmoid(x)


def _log_sigmoid(x):
    return jnp.minimum(x, 0.0) - jnp.log(1.0 + jnp.exp(-jnp.abs(x)))


def _cparams(sem):
    return pltpu.CompilerParams(dimension_semantics=sem, vmem_limit_bytes=VMEM_LIMIT)


def _ada_kernel(c_ref, w_ref, b_ref, o_ref):
    ca = _silu(c_ref[...])
    o_ref[0] = jnp.dot(ca, w_ref[0], preferred_element_type=F32) + b_ref[0]


def _ada_call(c, w_ada, b_ada):
    depth, d, n = w_ada.shape
    b = c.shape[0]
    return pl.pallas_call(
        _ada_kernel,
        grid=(depth, n // ADA_TN),
        in_specs=[
            pl.BlockSpec((b, d), lambda l, j: (0, 0)),
            pl.BlockSpec((1, d, ADA_TN), lambda l, j: (l, 0, j)),
            pl.BlockSpec((1, 1, ADA_TN), lambda l, j: (l, 0, j)),
        ],
        out_specs=pl.BlockSpec((1, b, ADA_TN), lambda l, j: (l, 0, j)),
        out_shape=jax.ShapeDtypeStruct((depth, b, n), F32),
        compiler_params=_cparams(("arbitrary", "arbitrary")),
        name="adaln_mod",
    )(c, w_ada, b_ada.reshape(depth, 1, n))


def _t5_bucket_np(n):
    max_exact = REL_BUCKETS // 2
    nf = np.maximum(n, 1).astype(np.float32)
    large = max_exact + (np.log(nf / np.float32(max_exact)) / np.float32(math.log(REL_MAX_DIST / max_exact))
                         * np.float32(REL_BUCKETS - max_exact)).astype(np.int32)
    large = np.minimum(large, REL_BUCKETS - 1)
    return np.where(n < max_exact, n, large).astype(np.int32)


def _bias_kernel(tbl_ref, bk_ref, o_ref):
    h = pl.program_id(0)
    far = tbl_ref[REL_BUCKETS - 1, h]
    for t in range(2):
        bk = bk_ref[t]
        acc = jnp.zeros(bk.shape, F32)
        for b in range(REL_BUCKETS - 1):
            acc = jnp.where(bk == b, tbl_ref[b, h] - far, acc)
        acc = jnp.where(bk < 0, -jnp.inf, acc)
        o_ref[0, t] = acc


def _bias_call(rel_bias, t):
    r = np.arange(t)[:, None]
    c = np.arange(t)[None, :]
    d_prev = r - c + t
    d_diag = r - c
    bk = np.stack([_t5_bucket_np(d_prev), np.where(d_diag >= 0, _t5_bucket_np(np.maximum(d_diag, 0)), -1)])
    return pl.pallas_call(
        _bias_kernel,
        grid=(A_HEADS,),
        in_specs=[
            pl.BlockSpec(memory_space=pltpu.SMEM),
            pl.BlockSpec((2, t, t), lambda h: (0, 0, 0)),
        ],
        out_specs=pl.BlockSpec((1, 2, t, t), lambda h: (h, 0, 0, 0)),
        out_shape=jax.ShapeDtypeStruct((A_HEADS, 2, t, t), F32),
        compiler_params=_cparams(("arbitrary",)),
        name="t5_bias_tiles",
    )(rel_bias, jnp.asarray(bk, jnp.int32))


def _inproj_kernel(x_ref, g_ref, scale_ref, shift_ref, w_ref, wif_ref, o_ref, oif_ref, h_ref):
    @pl.when(pl.program_id(1) == 0)
    def _():
        x = x_ref[...]
        ms = jnp.mean(x * x, axis=-1, keepdims=True)
        y = x * lax.rsqrt(ms + EPS) * g_ref[...]
        h = (y * (1.0 + scale_ref[0]) + shift_ref[0]).astype(BF16)
        h_ref[...] = h
        oif_ref[...] = jnp.dot(h, wif_ref[...], preferred_element_type=F32)

    o_ref[...] = jnp.dot(h_ref[...], w_ref[...], preferred_element_type=F32).astype(BF16)


def _inproj_call(x2, norm_g, mod3, w_main, w_if, seq):
    m, d = x2.shape
    tm = min(PROJ_TM, seq)
    tiles_per_seq = seq // tm
    return pl.pallas_call(
        _inproj_kernel,
        grid=(m // tm, N_MAIN // PROJ_TN),
        in_specs=[
            pl.BlockSpec((tm, d), lambda i, j: (i, 0)),
            pl.BlockSpec((1, d), lambda i, j: (0, 0)),
            pl.BlockSpec((1, 1, d), lambda i, j: (i // tiles_per_seq, 0, 1)),
            pl.BlockSpec((1, 1, d), lambda i, j: (i // tiles_per_seq, 0, 0)),
            pl.BlockSpec((d, PROJ_TN), lambda i, j: (0, j)),
            pl.BlockSpec((d, IF_PAD), lambda i, j: (0, 0)),
        ],
        out_specs=[
            pl.BlockSpec((tm, PROJ_TN), lambda i, j: (i, j)),
            pl.BlockSpec((tm, IF_PAD), lambda i, j: (i, 0)),
        ],
        out_shape=[
            jax.ShapeDtypeStruct((m, N_MAIN), BF16),
            jax.ShapeDtypeStruct((m, IF_PAD), F32),
        ],
        scratch_shapes=[pltpu.VMEM((tm, d), BF16)],
        compiler_params=_cparams(("arbitrary", "arbitrary")),
        name="norm_inproj",
    )(x2, norm_g.reshape(1, d), mod3, mod3, w_main, w_if)


def _mlstm_kernel(q_ref, k_ref, v_ref, o_ref, z_ref, gates_ref, convw_ref, convb_ref, bif_ref, mg_ref,
                  out_ref, qkbuf, qkact, c_state, n_state, m_state, *, t_block):
    halo = 8
    nblk = 2 * M_HEADS

    @pl.when(pl.program_id(1) == 0)
    def _():
        qkbuf[:, 0:halo, :] = jnp.zeros((nblk, halo, M_DK), F32)
        c_state[...] = jnp.zeros(c_state.shape, F32)
        n_state[...] = jnp.zeros(n_state.shape, F32)
        m_state[...] = jnp.zeros(m_state.shape, F32)

    @pl.when(pl.program_id(1) > 0)
    def _():
        qkbuf[:, 0:halo, :] = qkbuf[:, t_block:t_block + halo, :]

    for hb in range(M_HEADS):
        qkbuf[hb, halo:halo + t_block, :] = q_ref[:, hb * M_DK:(hb + 1) * M_DK].astype(F32)
        qkbuf[M_HEADS + hb, halo:halo + t_block, :] = k_ref[:, hb * M_DK:(hb + 1) * M_DK].astype(F32)

    def conv_body(cb, carry):
        w = convw_ref[cb]
        acc = qkbuf[cb, halo:halo + t_block, :] * w[CONV_K - 1:CONV_K, :] + convb_ref[cb]
        for j in range(CONV_K - 1):
            lo = halo - (CONV_K - 1) + j
            acc = acc + qkbuf[cb, lo:lo + t_block, :] * w[j:j + 1, :]
        act = _silu(acc)
        act = jnp.where(cb < M_HEADS, act * (M_DK ** -0.5), act)
        qkact[cb] = act.astype(BF16)
        return carry

    lax.fori_loop(0, nblk, conv_body, 0)

    row = lax.broadcasted_iota(jnp.int32, (M_CHUNK, M_CHUNK), 0)
    col = lax.broadcasted_iota(jnp.int32, (M_CHUNK, M_CHUNK), 1)
    causal = row >= col
    tril = causal.astype(F32)

    def chunk_body(c, carry):
        r0 = pl.multiple_of(c * M_CHUNK, M_CHUNK)
        rows = pl.ds(r0, M_CHUNK)
        gates = gates_ref[rows, :] + bif_ref[...]
        bcum = jnp.dot(tril, _log_sigmoid(gates), preferred_element_type=F32,
                       precision=lax.Precision.HIGHEST)
        xg = jnp.where(col < M_HEADS, gates, bcum)
        xgt = xg.T
        for h in range(M_HEADS):
            q = qkact[h, rows, :]
            k = qkact[M_HEADS + h, rows, :]
            v = v_ref[rows, h * M_DV:(h + 1) * M_DV]
            i_row = xgt[h:h + 1, :]
            b_row = xgt[M_HEADS + h:M_HEADS + h + 1, :]
            i_col = xg[:, h:h + 1]
            b_col = xg[:, M_HEADS + h:M_HEADS + h + 1]
            m_prev = m_state[h]
            dmat = jnp.where(causal, b_col - b_row + i_row, -jnp.inf)
            inter = b_col + m_prev
            m_t = jnp.maximum(inter, jnp.max(dmat, axis=-1, keepdims=True))
            w_inter = jnp.exp(inter - m_t)
            s = lax.dot_general(q, k, (((1,), (1,)), ((), ())), preferred_element_type=F32)
            s = s * jnp.exp(dmat - m_t)
            c_old = c_state[h]
            num = (w_inter * jnp.dot(q, c_old.astype(BF16), preferred_element_type=F32)
                   + jnp.dot(s.astype(BF16), v, preferred_element_type=F32))
            n_old = n_state[h]
            qn = jnp.sum(q.astype(F32) * n_old, axis=-1, keepdims=True)
            den = w_inter * qn + jnp.sum(s, axis=-1, keepdims=True)
            hh = num / jnp.maximum(jnp.abs(den), jnp.exp(-m_t))
            b_last = b_col[M_CHUNK - 1:M_CHUNK, :]
            g_col = b_last - b_col + i_col
            m_new = jnp.maximum(b_last + m_prev, jnp.max(g_col, axis=0, keepdims=True))
            decay = jnp.exp(b_last + m_prev - m_new)
            kw = k.astype(F32) * jnp.exp(g_col - m_new)
            c_state[h] = decay * c_old + lax.dot_general(
                kw.astype(BF16), v, (((0,), (0,)), ((), ())), preferred_element_type=F32)
            n_state[h] = decay * n_old + jnp.sum(kw, axis=0, keepdims=True)
            m_state[h] = m_new
            cols = slice(h * M_DV, (h + 1) * M_DV)
            hm = jax.nn.sigmoid(o_ref[rows, cols].astype(F32)) * hh
            hm = hm * lax.rsqrt(jnp.mean(hm * hm, axis=-1, keepdims=True) + EPS) * mg_ref[:, cols]
            out_ref[rows, cols] = (hm * _silu(z_ref[rows, cols].astype(F32))).astype(BF16)
        return carry

    lax.fori_loop(0, t_block // M_CHUNK, chunk_body, 0)


def _mlstm_call(proj, gates, conv_w, conv_b, b_if, mnorm_g, batch, seq):
    m = proj.shape[0]
    t = min(MLSTM_T, seq)
    nt = seq // t
    nblk = 2 * M_HEADS
    convw3 = conv_w.reshape(CONV_K, nblk, M_DK).transpose(1, 0, 2)
    convb3 = conv_b.reshape(nblk, 1, M_DK)
    bif = jnp.zeros((1, IF_PAD), F32).at[0, :2 * M_HEADS].set(b_if)
    row = lambda b, s: b * nt + s
    return pl.pallas_call(
        functools.partial(_mlstm_kernel, t_block=t),
        grid=(batch, nt),
        in_specs=[
            pl.BlockSpec((t, M_QK_WIDTH), lambda b, s: (row(b, s), COL_QM // M_QK_WIDTH)),
            pl.BlockSpec((t, M_QK_WIDTH), lambda b, s: (row(b, s), COL_KM // M_QK_WIDTH)),
            pl.BlockSpec((t, M_V_WIDTH), lambda b, s: (row(b, s), COL_VM // M_V_WIDTH)),
            pl.BlockSpec((t, M_V_WIDTH), lambda b, s: (row(b, s), COL_OM // M_V_WIDTH)),
            pl.BlockSpec((t, M_V_WIDTH), lambda b, s: (row(b, s), COL_ZM // M_V_WIDTH)),
            pl.BlockSpec((t, IF_PAD), lambda b, s: (row(b, s), 0)),
            pl.BlockSpec((nblk, CONV_K, M_DK), lambda b, s: (0, 0, 0)),
            pl.BlockSpec((nblk, 1, M_DK), lambda b, s: (0, 0, 0)),
            pl.BlockSpec((1, IF_PAD), lambda b, s: (0, 0)),
            pl.BlockSpec((1, M_V_WIDTH), lambda b, s: (0, 0)),
        ],
        out_specs=pl.BlockSpec((t, M_V_WIDTH), lambda b, s: (row(b, s), 0)),
        out_shape=jax.ShapeDtypeStruct((m, M_V_WIDTH), BF16),
        scratch_shapes=[
            pltpu.VMEM((nblk, t + 8, M_DK), F32),
            pltpu.VMEM((nblk, t, M_DK), BF16),
            pltpu.VMEM((M_HEADS, M_DK, M_DV), F32),
            pltpu.VMEM((M_HEADS, 1, M_DK), F32),
            pltpu.VMEM((M_HEADS, 1, 1), F32),
        ],
        compiler_params=_cparams(("arbitrary", "arbitrary")),
        name="mlstm",
    )(proj, proj, proj, proj, proj, gates, convw3, convb3, bif, mnorm_g.reshape(1, M_V_WIDTH))


def _attn_kernel(lq1_ref, lk1_ref, lq2_ref, lk2_ref, q_ref, k_ref, v_ref, z_ref, bias_ref, g_ref, out_ref,
                 *, t, lam_init):
    i = pl.program_id(2)
    lam = (jnp.exp(jnp.sum(lq1_ref[...] * lk1_ref[...], axis=-1, keepdims=True))
           - jnp.exp(jnp.sum(lq2_ref[...] * lk2_ref[...], axis=-1, keepdims=True)) + lam_init)

    lane = lax.broadcasted_iota(jnp.int32, (t, A_HEAD_WIDTH), 1)
    qs = q_ref[...] * (A_DH ** -0.5)
    zero = jnp.zeros_like(qs)
    q1 = jnp.where(lane < A_DH, qs, zero)
    q2 = jnp.where(lane >= A_DH, qs, zero)

    def online(q, kb, vb, bias, m, l, acc):
        s = lax.dot_general(q, kb, (((1,), (1,)), ((), ())), preferred_element_type=F32)
        if bias is not None:
            s = s + bias
        m_new = jnp.maximum(m, jnp.max(s, axis=-1, keepdims=True))
        alpha = jnp.exp(m - m_new)
        p = jnp.exp(s - m_new)
        l = alpha * l + jnp.sum(p, axis=-1, keepdims=True)
        acc = alpha * acc + jnp.dot(p.astype(BF16), vb, preferred_element_type=F32)
        return m_new, l, acc

    def step(j, bias, carry):
        r0 = pl.multiple_of(j * t, t)
        kb = k_ref[pl.ds(r0, t), :]
        vb = v_ref[pl.ds(r0, t), :]
        m1, l1, a1, m2, l2, a2 = carry
        m1, l1, a1 = online(q1, kb, vb, bias, m1, l1, a1)
        m2, l2, a2 = online(q2, kb, vb, bias, m2, l2, a2)
        return m1, l1, a1, m2, l2, a2

    neg = jnp.full((t, 1), -jnp.inf, F32)
    zl = jnp.zeros((t, 1), F32)
    za = jnp.zeros((t, A_HEAD_WIDTH), F32)
    carry = (neg, zl, za, neg, zl, za)
    carry = lax.fori_loop(0, jnp.maximum(i - 1, 0), lambda j, cr: step(j, None, cr), carry)
    carry = lax.cond(i >= 1, lambda cr: step(i - 1, bias_ref[0, 0], cr), lambda cr: cr, carry)
    m1, l1, a1, m2, l2, a2 = step(i, bias_ref[0, 1], carry)

    o = a1 / l1 - lam * (a2 / l2)
    y = o * lax.rsqrt(jnp.mean(o * o, axis=-1, keepdims=True) + EPS) * g_ref[...]
    y = y * (1.0 - lam_init)
    out_ref[...] = (y * _silu(z_ref[...].astype(F32))).astype(BF16)


def _attn_call(proj, lam_q1, lam_k1, lam_q2, lam_k2, bias_tiles, anorm_g, lam_init, batch, seq):
    m = proj.shape[0]
    t = min(ATTN_T, seq)
    nt = seq // t
    hw = A_HEAD_WIDTH
    lam_spec = pl.BlockSpec((1, A_DH), lambda b, h, i: (0, 0))
    return pl.pallas_call(
        functools.partial(_attn_kernel, t=t, lam_init=lam_init),
        grid=(batch, A_HEADS, nt),
        in_specs=[
            lam_spec, lam_spec, lam_spec, lam_spec,
            pl.BlockSpec((t, hw), lambda b, h, i: (b * nt + i, COL_QA // hw + h)),
            pl.BlockSpec((seq, hw), lambda b, h, i: (b, COL_KA // hw + h)),
            pl.BlockSpec((seq, hw), lambda b, h, i: (b, COL_VA // hw + h)),
            pl.BlockSpec((t, hw), lambda b, h, i: (b * nt + i, COL_ZA // hw + h)),
            pl.BlockSpec((1, 2, t, t), lambda b, h, i: (h, 0, 0, 0)),
            pl.BlockSpec((1, hw), lambda b, h, i: (0, h)),
        ],
        out_specs=pl.BlockSpec((t, hw), lambda b, h, i: (b * nt + i, h)),
        out_shape=jax.ShapeDtypeStruct((m, A_WIDTH), BF16),
        compiler_params=_cparams(("arbitrary", "arbitrary", "arbitrary")),
        name="diff_attn",
    )(lam_q1.reshape(1, A_DH), lam_k1.reshape(1, A_DH), lam_q2.reshape(1, A_DH), lam_k2.reshape(1, A_DH),
      proj, proj, proj, proj, bias_tiles, anorm_g.reshape(1, A_WIDTH))


def _merge_kernel(hm_ref, ha_ref, gm_ref, ga_ref, x_ref, gate_ref, wm_ref, wa_ref, wo_ref, fg_ref, out_ref,
                  *, final_norm):
    ym = jnp.dot(hm_ref[...], wm_ref[...], preferred_element_type=F32)
    ya = jnp.dot(ha_ref[...], wa_ref[...], preferred_element_type=F32)
    y = (jax.nn.sigmoid(gm_ref[...].astype(F32)) * ym + jax.nn.sigmoid(ga_ref[...].astype(F32)) * ya)
    out = jnp.dot(y.astype(BF16), wo_ref[...], preferred_element_type=F32)
    xn = x_ref[...] + gate_ref[0] * out
    if final_norm:
        xn = xn * lax.rsqrt(jnp.mean(xn * xn, axis=-1, keepdims=True) + EPS) * fg_ref[...]
    out_ref[...] = xn


def _merge_call(hm, ha, proj, x2, mod3, w_m, w_a, w_o, final_g, final_norm, seq):
    m, d = x2.shape
    tm = min(MERGE_TM, seq)
    tiles_per_seq = seq // tm
    const = lambda i: (0, 0)
    return pl.pallas_call(
        functools.partial(_merge_kernel, final_norm=final_norm),
        grid=(m // tm,),
        in_specs=[
            pl.BlockSpec((tm, M_V_WIDTH), lambda i: (i, 0)),
            pl.BlockSpec((tm, A_WIDTH), lambda i: (i, 0)),
            pl.BlockSpec((tm, d), lambda i: (i, COL_GM // d)),
            pl.BlockSpec((tm, d), lambda i: (i, COL_GA // d)),
            pl.BlockSpec((tm, d), lambda i: (i, 0)),
            pl.BlockSpec((1, 1, d), lambda i: (i // tiles_per_seq, 0, 2)),
            pl.BlockSpec((M_V_WIDTH, d), const),
            pl.BlockSpec((A_WIDTH, d), const),
            pl.BlockSpec((d, d), const),
            pl.BlockSpec((1, d), const),
        ],
        out_specs=pl.BlockSpec((tm, d), lambda i: (i, 0)),
        out_shape=jax.ShapeDtypeStruct((m, d), F32),
        compiler_params=_cparams(("arbitrary",)),
        name="merge_final" if final_norm else "merge",
    )(hm, ha, proj, proj, x2, mod3, w_m, w_a, w_o, final_g.reshape(1, d))


def kernel(x, c, norm_g, w_ada, b_ada, w_in, b_if, conv_w, conv_b, mnorm_g, w_proj_m, lam_q1, lam_k1, lam_q2,
           lam_k2, anorm_g, w_proj_a, w_out, rel_bias, final_g):
    batch, seq, d = x.shape
    depth = w_in.shape[0]
    assert d == D_MODEL and seq % M_CHUNK == 0
    mod = _ada_call(c, w_ada, b_ada)
    bias_tiles = _bias_call(rel_bias, min(ATTN_T, seq))
    x2 = x.reshape(batch * seq, d)
    for l in range(depth):
        w_l = w_in[l]
        w_main = jnp.concatenate([w_l[:, :COL_IF], w_l[:, COL_IF + 2 * M_HEADS:]], axis=1).astype(BF16)
        w_if = jnp.pad(w_l[:, COL_IF:COL_IF + 2 * M_HEADS], ((0, 0), (0, IF_PAD - 2 * M_HEADS))).astype(BF16)
        mod3 = mod[l].reshape(batch, 1, 3 * d)
        proj, gates = _inproj_call(x2, norm_g[l], mod3, w_main, w_if, seq)
        hm = _mlstm_call(proj, gates, conv_w[l], conv_b[l], b_if[l], mnorm_g[l], batch, seq)
        lam_init = 0.8 - 0.6 * math.exp(-0.3 * l)
        ha = _attn_call(proj, lam_q1[l], lam_k1[l], lam_q2[l], lam_k2[l], bias_tiles, anorm_g[l], lam_init,
                        batch, seq)
        x2 = _merge_call(hm, ha, proj, x2, mod3, w_proj_m[l].astype(BF16), w_proj_a[l].astype(BF16),
                         w_out[l].astype(BF16), final_g, l == depth - 1, seq)
    return x2.reshape(batch, seq, d)
```
